```python
import jax, jax.numpy as jnp
from jax import lax
import numpy as np

D_MODEL = 2048
BATCH = 8
SEQ = 2048
DEPTH = 4
DEC_BATCH = 32
DEC_SEQ = 64
PAST_LEN = 4096

CHUNK = 64
BRANCH_WIDTH = D_MODEL // 2
N_BRANCH = 3
GLA_HEADS = 4
GLA_DV = BRANCH_WIDTH // GLA_HEADS
GLA_DK = GLA_DV // 2
GLA_GATE_RANK = 16
GLA_TAU = 16.0
RMS_EPS = 1e-6
POOL_WINDOWS = (2, 4, 8, 16)
N_POOL = len(POOL_WINDOWS)
POOL_GROUP = BRANCH_WIDTH // N_POOL
POOL_HIST = max(POOL_WINDOWS) - 1
MEM_LEN = 256
MEM_HEADS = 4
MEM_HD = BRANCH_WIDTH // MEM_HEADS
N_EXPERTS = 64
TOP_K = 8
N_GROUPS = 8
TOPK_GROUPS = 4
D_EXPERT = D_MODEL // 4
ROUTED_SCALE = 2.5
EXPERT_BLOCK = 128
ALPHA = (2 * DEPTH) ** 0.25
BETA = (8 * DEPTH) ** -0.25
LN_EPS = 1e-5
PROJ_SIZES = (GLA_HEADS * GLA_DK, GLA_HEADS * GLA_DK, GLA_HEADS * GLA_DV, GLA_HEADS * GLA_DV,
              GLA_GATE_RANK, BRANCH_WIDTH, MEM_HEADS * MEM_HD, N_BRANCH * D_MODEL)
PROJ_WIDTH = sum(PROJ_SIZES)

kernel_name = 'hybrid_gla_pool_mem_moe_stream_step'


def layer_norm(x, g, b):
    xf = x.astype(jnp.float32)
    mu = jnp.mean(xf, axis=-1, keepdims=True)
    var = jnp.mean(jnp.square(xf - mu), axis=-1, keepdims=True)
    return ((xf - mu) * lax.rsqrt(var + LN_EPS) * g + b).astype(x.dtype)


def split_proj(z):
    out, off = [], 0
    for n in PROJ_SIZES:
        out.append(z[..., off:off + n])
        off += n
    return out


def gla_scan(q, k, v, lg, s0):
    B, L, H, DK = q.shape
    DV = v.shape[-1]
    c = min(CHUNK, L)
    n = L // c
    f32 = jnp.float32
    q = q.astype(f32).reshape(B, n, c, H, DK)
    k = k.astype(f32).reshape(B, n, c, H, DK)
    v = v.astype(f32).reshape(B, n, c, H, DV)
    cum = jnp.cumsum(lg.astype(f32).reshape(B, n, c, H, DK), axis=2)
    last = cum[:, :, -1]
    q_dec = q * jnp.exp(cum)
    k_inv = k * jnp.exp(-cum)
    k_end = k * jnp.exp(last[:, :, None] - cum)
    causal = jnp.tril(jnp.ones((c, c), dtype=bool))
    att = jnp.where(causal, jnp.einsum('bnthk,bnshk->bnhts', q_dec, k_inv), 0.0)
    o = jnp.einsum('bnhts,bnshv->bnthv', att, v)
    ds = jnp.einsum('bnshk,bnshv->nbhkv', k_end, v)
    dec = jnp.exp(last).transpose(1, 0, 2, 3)

    def step(s, inp):
        ds_c, dec_c = inp
        return s * dec_c[..., None] + ds_c, s

    s_fin, s_start = lax.scan(step, s0.astype(f32), (ds, dec))
    o = o + jnp.einsum('bnthk,nbhkv->bnthv', q_dec, s_start)
    return o.reshape(B, L, H, DV), s_fin


def multiscale_pool(u_ext, pos0):
    L = u_ext.shape[1] - POOL_HIST
    uf = u_ext.astype(jnp.float32)
    c = jnp.cumsum(uf, axis=1)
    c = jnp.concatenate([jnp.zeros_like(c[:, :1]), c], axis=1)
    pos = pos0 + jnp.arange(L)
    outs = []
    for g, w in enumerate(POOL_WINDOWS):
        cg = c[..., g * POOL_GROUP:(g + 1) * POOL_GROUP]
        s = cg[:, POOL_HIST + 1:] - cg[:, POOL_HIST + 1 - w:POOL_HIST + 1 - w + L]
        cnt = jnp.minimum(pos + 1, w).astype(jnp.float32)
        outs.append(s / cnt[None, :, None])
    mean = jnp.concatenate(outs, axis=-1)
    return (mean - uf[:, POOL_HIST:]).astype(u_ext.dtype)


def memory_attention(qm, mem_k, mem_v):
    s = jnp.einsum('blhd,bmhd->bhlm', qm, mem_k).astype(jnp.float32) * MEM_HD ** -0.5
    p = jax.nn.softmax(s, axis=-1).astype(mem_v.dtype)
    return jnp.einsum('bhlm,bmhd->blhd', p, mem_v)


def mixer(x, pos0, s_gla, pool_hist, mem_k, mem_v, w_in, w_a2, b_a2, gla_norm_g, w_pool, pool_scale, w_branch, w_o):
    B, L, _ = x.shape
    q, k, v, r, a, u, qm, gz = split_proj(x @ w_in)
    q = q.reshape(B, L, GLA_HEADS, GLA_DK) * GLA_DK ** -0.5
    k = k.reshape(B, L, GLA_HEADS, GLA_DK)
    v = v.reshape(B, L, GLA_HEADS, GLA_DV)
    lg = jax.nn.log_sigmoid((a @ w_a2 + b_a2).astype(jnp.float32)) / GLA_TAU
    o, s_new = gla_scan(q, k, v, lg.reshape(B, L, GLA_HEADS, GLA_DK), s_gla)
    o = o * lax.rsqrt(jnp.mean(jnp.square(o), axis=-1, keepdims=True) + RMS_EPS)
    o = o.reshape(B, L, BRANCH_WIDTH) * gla_norm_g
    br_gla = jax.nn.silu(r) * o.astype(x.dtype)
    u_ext = jnp.concatenate([pool_hist.astype(u.dtype), u], axis=1)
    p = multiscale_pool(u_ext, pos0).reshape(B, L, N_POOL, POOL_GROUP)
    br_pool = jnp.einsum('blgc,gcd->blgd', p, w_pool).reshape(B, L, BRANCH_WIDTH) * pool_scale
    br_mem = memory_attention(qm.reshape(B, L, MEM_HEADS, MEM_HD), mem_k, mem_v).reshape(B, L, BRANCH_WIDTH)
    gates = jax.nn.sigmoid(gz).reshape(B, L, N_BRANCH, D_MODEL)
    merged = (gates[:, :, 0] * (br_gla @ w_branch[0])
              + gates[:, :, 1] * (br_pool @ w_branch[1])
              + gates[:, :, 2] * (br_mem @ w_branch[2]))
    return merged @ w_o, s_new.astype(x.dtype), u_ext[:, -POOL_HIST:]


def route(h, w_router, router_bias):
    T = h.shape[0]
    s = jax.nn.sigmoid((h @ w_router).astype(jnp.float32))
    sel = s + router_bias.astype(jnp.float32)
    grp = sel.reshape(T, N_GROUPS, N_EXPERTS // N_GROUPS)
    gscore = jnp.sum(lax.top_k(grp, 2)[0], axis=-1)
    _, gidx = lax.top_k(gscore, TOPK_GROUPS)
    gmask = jnp.sum(jax.nn.one_hot(gidx, N_GROUPS, dtype=jnp.float32), axis=1) > 0
    emask = jnp.repeat(gmask, N_EXPERTS // N_GROUPS, axis=1)
    _, eidx = lax.top_k(jnp.where(emask, sel, -jnp.inf), TOP_K)
    w = jnp.take_along_axis(s, eidx, axis=1)
    w = w / jnp.sum(w, axis=-1, keepdims=True) * ROUTED_SCALE
    return eidx.astype(jnp.int32), w


def routed_experts(h, eidx, ew, w_e1, w_e3, w_e2):
    T, D = h.shape
    A = T * TOP_K
    flat_e = eidx.reshape(A)
    flat_t = jnp.arange(A, dtype=jnp.int32) // TOP_K
    flat_w = ew.reshape(A)
    order = jnp.argsort(flat_e)
    se = flat_e[order]
    counts = jnp.bincount(flat_e, length=N_EXPERTS)
    padded = (counts + EXPERT_BLOCK - 1) // EXPERT_BLOCK * EXPERT_BLOCK
    pend = jnp.cumsum(padded)
    pstart = pend - padded
    start = jnp.cumsum(counts) - counts
    slot = pstart[se] + (jnp.arange(A, dtype=jnp.int32) - start[se])
    n_blocks = -(-A // EXPERT_BLOCK) + N_EXPERTS
    n_slots = n_blocks * EXPERT_BLOCK
    slot_tok = jnp.full((n_slots,), T, jnp.int32).at[slot].set(flat_t[order])
    slot_w = jnp.zeros((n_slots,), jnp.float32).at[slot].set(flat_w[order])
    block_e = jnp.minimum(jnp.searchsorted(pend, jnp.arange(n_blocks, dtype=jnp.int32) * EXPERT_BLOCK, side='right'),
                          N_EXPERTS - 1)
    hp = jnp.concatenate([h, jnp.zeros((1, D), h.dtype)], axis=0)

    def one_block(args):
        toks, e = args
        xb = hp[toks]
        return (jax.nn.silu(xb @ w_e1[e]) * (xb @ w_e3[e])) @ w_e2[e]

    yb = lax.map(one_block, (slot_tok.reshape(n_blocks, EXPERT_BLOCK), block_e))
    yb = yb.reshape(n_slots, D).astype(jnp.float32) * slot_w[:, None]
    return jnp.zeros((T + 1, D), jnp.float32).at[slot_tok].add(yb)[:T].astype(h.dtype)


def moe_ffn(x, w_router, router_bias, w_e1, w_e3, w_e2, w_s1, w_s3, w_s2):
    B, L, D = x.shape
    h = x.reshape(B * L, D)
    eidx, ew = route(h, w_router, router_bias)
    shared = (jax.nn.silu(h @ w_s1) * (h @ w_s3)) @ w_s2
    return (shared + routed_experts(h, eidx, ew, w_e1, w_e3, w_e2)).reshape(B, L, D)


def encoder_layer(x, pos0, s_gla, pool_hist, mem_k, mem_v, w_in, w_a2, b_a2, gla_norm_g, w_pool, pool_scale,
                  w_branch, w_o, ln1_g, ln1_b, w_router, router_bias, w_e1, w_e3, w_e2, w_s1, w_s3, w_s2, ln2_g, ln2_b):
    y, s_new, hist_new = mixer(x, pos0, s_gla, pool_hist, mem_k, mem_v, w_in, w_a2, b_a2, gla_norm_g,
                               w_pool, pool_scale, w_branch, w_o)
    x = layer_norm(ALPHA * x + y, ln1_g, ln1_b)
    x = layer_norm(ALPHA * x + moe_ffn(x, w_router, router_bias, w_e1, w_e3, w_e2, w_s1, w_s3, w_s2), ln2_g, ln2_b)
    return x, s_new, hist_new


def setup_inputs(seed: int = 0) -> dict:
    key = jax.random.key(seed)
    ks = jax.random.split(key, 32)
    D = D_MODEL

    def nrm(i, shape, scale=1.0):
        return jax.random.normal(ks[i], shape, jnp.float32) * scale

    return {
        'x_prompt': nrm(0, (BATCH, SEQ, D)),
        'x_sample': nrm(1, (DEC_BATCH, DEC_SEQ, D)),
        'mem_prompt': nrm(2, (BATCH, MEM_LEN, D)),
        'state_gla': nrm(3, (DEPTH, DEC_BATCH, GLA_HEADS, GLA_DK, GLA_DV)),
        'state_pool': nrm(4, (DEPTH, DEC_BATCH, POOL_HIST, BRANCH_WIDTH)),
        'cache_mem_k': nrm(5, (DEPTH, DEC_BATCH, MEM_LEN, MEM_HEADS, MEM_HD)),
        'cache_mem_v': nrm(6, (DEPTH, DEC_BATCH, MEM_LEN, MEM_HEADS, MEM_HD)),
        'ln_in_g': 1.0 + nrm(7, (D,), 0.02),
        'ln_in_b': nrm(8, (D,), 0.02),
        'w_in': nrm(9, (DEPTH, D, PROJ_WIDTH), D ** -0.5),
        'w_a2': nrm(10, (DEPTH, GLA_GATE_RANK, GLA_HEADS * GLA_DK), GLA_GATE_RANK ** -0.5),
        'b_a2': nrm(11, (DEPTH, GLA_HEADS * GLA_DK), 0.02),
        'gla_norm_g': 1.0 + nrm(12, (DEPTH, BRANCH_WIDTH), 0.02),
        'w_pool': nrm(13, (DEPTH, N_POOL, POOL_GROUP, POOL_GROUP), POOL_GROUP ** -0.5),
        'pool_scale': 1.0 + nrm(14, (DEPTH, BRANCH_WIDTH), 0.02),
        'w_mk': nrm(15, (DEPTH, D, BRANCH_WIDTH), D ** -0.5),
        'w_mv': nrm(16, (DEPTH, D, BRANCH_WIDTH), D ** -0.5),
        'w_branch': nrm(17, (DEPTH, N_BRANCH, BRANCH_WIDTH, D), BRANCH_WIDTH ** -0.5 * BETA),
        'w_o': nrm(18, (DEPTH, D, D), D ** -0.5 * BETA),
        'ln1_g': 1.0 + nrm(19, (DEPTH, D), 0.02),
        'ln1_b': nrm(20, (DEPTH, D), 0.02),
        'w_router': nrm(21, (DEPTH, D, N_EXPERTS), D ** -0.5),
        'router_bias': nrm(22, (DEPTH, N_EXPERTS), 0.01),
        'w_e1': nrm(23, (DEPTH, N_EXPERTS, D, D_EXPERT), D ** -0.5),
        'w_e3': nrm(24, (DEPTH, N_EXPERTS, D, D_EXPERT), D ** -0.5),
        'w_e2': nrm(25, (DEPTH, N_EXPERTS, D_EXPERT, D), D_EXPERT ** -0.5 * BETA),
        'w_s1': nrm(26, (DEPTH, D, D_EXPERT), D ** -0.5),
        'w_s3': nrm(27, (DEPTH, D, D_EXPERT), D ** -0.5),
        'w_s2': nrm(28, (DEPTH, D_EXPERT, D), D_EXPERT ** -0.5 * BETA),
        'ln2_g': 1.0 + nrm(29, (DEPTH, D), 0.02),
        'ln2_b': nrm(30, (DEPTH, D), 0.02),
    }


def reference(x_prompt, x_sample, mem_prompt, state_gla, state_pool, cache_mem_k, cache_mem_v,
              ln_in_g, ln_in_b, w_in, w_a2, b_a2, gla_norm_g, w_pool, pool_scale, w_mk, w_mv, w_branch, w_o,
              ln1_g, ln1_b, w_router, router_bias, w_e1, w_e3, w_e2, w_s1, w_s3, w_s2, ln2_g, ln2_b):
    xp = layer_norm(x_prompt, ln_in_g, ln_in_b)
    xs = layer_norm(x_sample, ln_in_g, ln_in_b)
    Bp = xp.shape[0]
    gla_p, pool_p, mk_p, mv_p, gla_s, pool_s = [], [], [], [], [], []
    for l in range(DEPTH):
        lw = (w_in[l], w_a2[l], b_a2[l], gla_norm_g[l], w_pool[l], pool_scale[l], w_branch[l], w_o[l],
              ln1_g[l], ln1_b[l], w_router[l], router_bias[l], w_e1[l], w_e3[l], w_e2[l],
              w_s1[l], w_s3[l], w_s2[l], ln2_g[l], ln2_b[l])
        mk = (mem_prompt @ w_mk[l]).reshape(Bp, mem_prompt.shape[1], MEM_HEADS, MEM_HD)
        mv = (mem_prompt @ w_mv[l]).reshape(Bp, mem_prompt.shape[1], MEM_HEADS, MEM_HD)
        s0 = jnp.zeros((Bp, GLA_HEADS, GLA_DK, GLA_DV), xp.dtype)
        h0 = jnp.zeros((Bp, POOL_HIST, BRANCH_WIDTH), xp.dtype)
        xp, sp, hp = encoder_layer(xp, 0, s0, h0, mk, mv, *lw)
        gla_p.append(sp)
        pool_p.append(hp)
        mk_p.append(mk)
        mv_p.append(mv)
        xs, ss, hs = encoder_layer(xs, PAST_LEN, state_gla[l], state_pool[l], cache_mem_k[l], cache_mem_v[l], *lw)
        gla_s.append(ss)
        pool_s.append(hs)
    return (xp, xs, jnp.stack(gla_p), jnp.stack(pool_p), jnp.stack(mk_p), jnp.stack(mv_p),
            jnp.stack(gla_s), jnp.stack(pool_s))
```

```python
import functools

import numpy as np
import jax
import jax.numpy as jnp
from jax import lax
from jax.experimental import pallas as pl
from jax.experimental.pallas import tpu as pltpu

F32 = jnp.float32
BF16 = jnp.bfloat16

CHUNK = 64
GLA_TAU = 16.0
RMS_EPS = 1e-6
LN_EPS = 1e-5
POOL_WINDOWS = (2, 4, 8, 16)
HIST_ROWS = 16
N_GROUPS = 8
TOPK_GROUPS = 4
TOP_K = 8
ROUTED_SCALE = 2.5
EXPERT_BLOCK = 128
A_PAD = 128
VMEM_LIMIT = 56 * 1024 * 1024

NT = (((1,), (1,)), ((), ()))
TN = (((0,), (0,)), ((), ()))


def _dot(a, b, dims=None):
    if dims is None:
        return jnp.dot(a, b, preferred_element_type=F32)
    return lax.dot_general(a, b, dims, preferred_element_type=F32)


def _split(x):
    hi = x.astype(BF16)
    lo = (x - hi.astype(F32)).astype(BF16)
    return hi, lo


def _sigmoid(x):
    return 1.0 / (1.0 + jnp.exp(-x))


def _silu(x):
    return x * _sigmoid(x)


def _layer_norm(x, g, b):
    mu = jnp.mean(x, axis=-1, keepdims=True)
    xc = x - mu
    var = jnp.mean(xc * xc, axis=-1, keepdims=True)
    return xc * lax.rsqrt(var + LN_EPS) * g + b


def _tile(n, pref):
    t = min(n, pref)
    while n % t:
        t //= 2
    return t


def _params(sem):
    return pltpu.CompilerParams(dimension_semantics=sem, vmem_limit_bytes=VMEM_LIMIT)


def _ln_kernel(x_ref, g_ref, b_ref, o_ref):
    o_ref[...] = _layer_norm(x_ref[...], g_ref[...], b_ref[...])


def _ln_call(x, g, b):
    T, D = x.shape
    tm = _tile(T, 512)
    return pl.pallas_call(
        _ln_kernel,
        grid=(T // tm,),
        in_specs=[pl.BlockSpec((tm, D), lambda i: (i, 0)),
                  pl.BlockSpec((1, D), lambda i: (0, 0)),
                  pl.BlockSpec((1, D), lambda i: (0, 0))],
        out_specs=pl.BlockSpec((tm, D), lambda i: (i, 0)),
        out_shape=jax.ShapeDtypeStruct((T, D), F32),
        compiler_params=_params(("parallel",)),
        name="ln_in",
    )(x, g.reshape(1, D), b.reshape(1, D))


def _mm_kernel(x_ref, w_ref, o_ref):
    o_ref[...] = _dot(x_ref[...].astype(BF16), w_ref[...])


def _mm_call(x, w, name):
    M, K = x.shape
    N = w.shape[1]
    tm, tn = _tile(M, 1024), _tile(N, 1024)
    return pl.pallas_call(
        _mm_kernel,
        grid=(M // tm, N // tn),
        in_specs=[pl.BlockSpec((tm, K), lambda i, j: (i, 0)),
                  pl.BlockSpec((K, tn), lambda i, j: (0, j))],
        out_specs=pl.BlockSpec((tm, tn), lambda i, j: (i, j)),
        out_shape=jax.ShapeDtypeStruct((M, N), F32),
        compiler_params=_params(("parallel", "arbitrary")),
        name=name,
    )(x, w)


def _proj_kernel(x_ref, w_ref, wa_ref, o_ref, a_ref, xb_ref):
    @pl.when(pl.program_id(1) == 0)
    def _():
        xb = x_ref[...].astype(BF16)
        xb_ref[...] = xb
        a_ref[...] = _dot(xb, wa_ref[...])

    o_ref[...] = _dot(xb_ref[...], w_ref[...]).astype(BF16)


def _proj_call(x, w_main, w_a):
    T, D = x.shape
    N = w_main.shape[1]
    tm, tn = _tile(T, 1024), _tile(N, 1024)
    return pl.pallas_call(
        _proj_kernel,
        grid=(T // tm, N // tn),
        in_specs=[pl.BlockSpec((tm, D), lambda i, j: (i, 0)),
                  pl.BlockSpec((D, tn), lambda i, j: (0, j)),
                  pl.BlockSpec((D, A_PAD), lambda i, j: (0, 0))],
        out_specs=[pl.BlockSpec((tm, tn), lambda i, j: (i, j)),
                   pl.BlockSpec((tm, A_PAD), lambda i, j: (i, 0))],
        out_shape=[jax.ShapeDtypeStruct((T, N), BF16),
                   jax.ShapeDtypeStruct((T, A_PAD), F32)],
        scratch_shapes=[pltpu.VMEM((tm, D), BF16)],
        compiler_params=_params(("parallel", "arbitrary")),
        name="in_proj",
    )(x, w_main, w_a)


def _mixer_kernel(seq_ref, pos_ref, first_ref, last_ref,
                  q_ref, k_ref, v_ref, r_ref, u_ref, qm_ref, a_ref, s0_ref, h0_ref, mk_ref, mv_ref,
                  wa2_ref, ba2_ref, gn_ref, wp_ref, ps_ref,
                  br_ref, sout_ref, hout_ref, s_ref, hist_ref, *, heads, dk, dv, mem_heads):
    i = pl.program_id(0)
    c = q_ref.shape[0]
    bw = heads * dv

    @pl.when(first_ref[i] == 1)
    def _():
        s_ref[...] = s0_ref[...]
        hist_ref[...] = h0_ref[...]

    row = lax.broadcasted_iota(jnp.int32, (c, c), 0)
    col = lax.broadcasted_iota(jnp.int32, (c, c), 1)
    causal = row >= col
    tri = jnp.where(causal, 1.0, 0.0).astype(BF16)

    pre = _dot(a_ref[...].astype(BF16), wa2_ref[...]) + ba2_ref[...]
    lg = (jnp.minimum(pre, 0.0) - jnp.log(1.0 + jnp.exp(-jnp.abs(pre)))) * (1.0 / GLA_TAU)
    lg_hi, lg_lo = _split(lg)
    cum = _dot(tri, lg_hi) + _dot(tri, lg_lo)
    last = cum[c - 1:c, :]
    qf = q_ref[...].astype(F32) * (dk ** -0.5)
    kf = k_ref[...].astype(F32)
    q_dec = (qf * jnp.exp(cum)).astype(BF16)
    k_inv = (kf * jnp.exp(-cum)).astype(BF16)
    k_end = (kf * jnp.exp(last - cum)).astype(BF16)
    ones_c = jnp.ones((c, 128), BF16)
    vb = v_ref[...]
    outs = []
    for h in range(heads):
        ks = slice(h * dk, (h + 1) * dk)
        vs = slice(h * dv, (h + 1) * dv)
        qd, ki, ke, vh = q_dec[:, ks], k_inv[:, ks], k_end[:, ks], vb[:, vs]
        s_h = s_ref[h]
        att = jnp.where(causal, _dot(qd, ki, NT), 0.0).astype(BF16)
        o = _dot(att, vh) + _dot(qd, s_h.astype(BF16))
        last_t = _dot(lg_hi[:, ks], ones_c, TN) + _dot(lg_lo[:, ks], ones_c, TN)
        dec = jnp.exp(last_t)
        dec = jnp.concatenate([dec] * (dv // 128), axis=1)
        s_ref[h] = s_h * dec + _dot(ke, vh, TN)
        o = o * lax.rsqrt(jnp.mean(o * o, axis=-1, keepdims=True) + RMS_EPS)
        outs.append(o)
    o = jnp.concatenate(outs, axis=1) * gn_ref[...]
    br_ref[:, 0:bw] = (_silu(r_ref[...].astype(F32)) * o).astype(BF16)

    uf = u_ref[...].astype(F32)
    ext = jnp.concatenate([hist_ref[...], uf], axis=0)
    ext_hi, ext_lo = _split(ext)
    trow = lax.broadcasted_iota(jnp.int32, (c, HIST_ROWS + c), 0) + HIST_ROWS
    jcol = lax.broadcasted_iota(jnp.int32, (c, HIST_ROWS + c), 1)
    pos = (pos_ref[i] + lax.broadcasted_iota(jnp.int32, (c, 1), 0) + 1).astype(F32)
    pg = bw // len(POOL_WINDOWS)
    pouts = []
    for g, w in enumerate(POOL_WINDOWS):
        cs = slice(g * pg, (g + 1) * pg)
        band = jnp.where((jcol <= trow) & (jcol > trow - w), 1.0, 0.0).astype(BF16)
        s = _dot(band, ext_hi[:, cs]) + _dot(band, ext_lo[:, cs])
        p = s / jnp.minimum(pos, float(w)) - uf[:, cs]
        pouts.append(_dot(p.astype(BF16), wp_ref[g]))
    br_ref[:, bw:2 * bw] = (jnp.concatenate(pouts, axis=1) * ps_ref[...]).astype(BF16)
    hist_ref[...] = ext[c:c + HIST_ROWS, :]

    hd = bw // mem_heads
    qm = qm_ref[...]
    mouts = []
    for h in range(mem_heads):
        hs = slice(h * hd, (h + 1) * hd)
        sc = _dot(qm[:, hs], mk_ref[:, hs], NT) * (hd ** -0.5)
        e = jnp.exp(sc - jnp.max(sc, axis=-1, keepdims=True))
        p = e / jnp.sum(e, axis=-1, keepdims=True)
        mouts.append(_dot(p.astype(BF16), mv_ref[:, hs]))
    br_ref[:, 2 * bw:3 * bw] = jnp.concatenate(mouts, axis=1).astype(BF16)

    @pl.when(last_ref[i] == 1)
    def _():
        sout_ref[...] = s_ref[...]
        hout_ref[...] = hist_ref[...]


def _mixer_call(meta, proj, a, s0, h0, mk, mv, wa2, ba2, gn, wp, ps, col_blocks, mem_heads):
    seq, pos0, first, last = meta
    T = proj.shape[0]
    nseq, heads, dk, dv = s0.shape
    bw = heads * dv
    mem_len = mk.shape[1]
    c = CHUNK
    cq, ck, cv, cr, cu, cqm = col_blocks
    kw = heads * dk

    def tok(width, blk):
        return pl.BlockSpec((c, width), lambda i, *_: (i, blk))

    def per_seq(shape):
        nd = len(shape)
        return pl.BlockSpec((None,) + shape, lambda i, sq, *_: (sq[i],) + (0,) * nd)

    def const(shape):
        nd = len(shape)
        return pl.BlockSpec(shape, lambda i, *_: (0,) * nd)

    kernel = functools.partial(_mixer_kernel, heads=heads, dk=dk, dv=dv, mem_heads=mem_heads)
    grid_spec = pltpu.PrefetchScalarGridSpec(
        num_scalar_prefetch=4,
        grid=(T // c,),
        in_specs=[tok(kw, cq), tok(kw, ck), tok(bw, cv), tok(bw, cr), tok(bw, cu), tok(bw, cqm),
                  tok(A_PAD, 0),
                  per_seq((heads, dk, dv)), per_seq((HIST_ROWS, bw)),
                  per_seq((mem_len, bw)), per_seq((mem_len, bw)),
                  const((A_PAD, kw)), const((1, kw)), const((1, bw)),
                  const(wp.shape), const((1, bw))],
        out_specs=[pl.BlockSpec((c, 3 * bw), lambda i, *_: (i, 0)),
                   per_seq((heads, dk, dv)), per_seq((HIST_ROWS, bw))],
        scratch_shapes=[pltpu.VMEM((heads, dk, dv), F32), pltpu.VMEM((HIST_ROWS, bw), F32)],
    )
    return pl.pallas_call(
        kernel,
        grid_spec=grid_spec,
        out_shape=[jax.ShapeDtypeStruct((T, 3 * bw), BF16),
                   jax.ShapeDtypeStruct(s0.shape, F32),
                   jax.ShapeDtypeStruct(h0.shape, F32)],
        compiler_params=_params(("arbitrary",)),
        name="seq_mixer",
    )(seq, pos0, first, last, proj, proj, proj, proj, proj, proj, a, s0, h0, mk, mv, wa2, ba2, gn, wp, ps)


def _merge_kernel(br_ref, gz_ref, x_ref, wb_ref, wo_ref, g_ref, b_ref, o_ref, *, alpha):
    nb, bw, d = wb_ref.shape
    merged = None
    for n in range(nb):
        y = _dot(br_ref[:, n * bw:(n + 1) * bw], wb_ref[n])
        t = _sigmoid(gz_ref[:, n * d:(n + 1) * d].astype(F32)) * y
        merged = t if merged is None else merged + t
    out = _dot(merged.astype(BF16), wo_ref[...])
    o_ref[...] = _layer_norm(alpha * x_ref[...] + out, g_ref[...], b_ref[...])


def _merge_call(br, proj, x, wb, wo, g, b, alpha):
    T, D = x.shape
    nb, bw, _ = wb.shape
    tm = _tile(T, 256)
    one = pl.Buffered(1)
    return pl.pallas_call(
        functools.partial(_merge_kernel, alpha=alpha),
        grid=(T // tm,),
        in_specs=[pl.BlockSpec((tm, nb * bw), lambda i: (i, 0)),
                  pl.BlockSpec((tm, nb * D), lambda i: (i, 0)),
                  pl.BlockSpec((tm, D), lambda i: (i, 0)),
                  pl.BlockSpec((nb, bw, D), lambda i: (0, 0, 0), pipeline_mode=one),
                  pl.BlockSpec((D, D), lambda i: (0, 0), pipeline_mode=one),
                  pl.BlockSpec((1, D), lambda i: (0, 0)),
                  pl.BlockSpec((1, D), lambda i: (0, 0))],
        out_specs=pl.BlockSpec((tm, D), lambda i: (i, 0)),
        out_shape=jax.ShapeDtypeStruct((T, D), F32),
        compiler_params=_params(("parallel",)),
        name="merge_out_ln1",
    )(br, proj, x, wb, wo, g.reshape(1, D), b.reshape(1, D))


def _router_kernel(x_ref, whi_ref, wlo_ref, bias_ref, eidx_ref, ew_ref, rank_ref, cnt_ref, carry_ref):
    i = pl.program_id(0)
    tm = x_ref.shape[0]
    ne = whi_ref.shape[0]
    gs = ne // N_GROUPS
    neg = -jnp.inf

    @pl.when(i == 0)
    def _():
        carry_ref[...] = jnp.zeros_like(carry_ref)

    h_hi, h_lo = _split(x_ref[...])
    logits = _dot(whi_ref[...], h_hi, NT) + _dot(whi_ref[...], h_lo, NT) + _dot(wlo_ref[...], h_hi, NT)
    s = _sigmoid(logits)
    sel = s + bias_ref[...][:, 0:1]
    s3 = s.reshape(N_GROUPS, gs, tm)
    g3 = sel.reshape(N_GROUPS, gs, tm)
    sub = lax.broadcasted_iota(jnp.int32, (N_GROUPS, gs, tm), 1)
    grp = lax.broadcasted_iota(jnp.int32, (N_GROUPS, gs, tm), 0)
    eio = grp * gs + sub

    m1 = jnp.max(g3, axis=1, keepdims=True)
    f1 = jnp.min(jnp.where(g3 == m1, sub, gs), axis=1, keepdims=True)
    m2 = jnp.max(jnp.where(sub == f1, neg, g3), axis=1, keepdims=True)
    gscore = (m1 + m2).reshape(N_GROUPS, tm)
    gio = lax.broadcasted_iota(jnp.int32, (N_GROUPS, tm), 0)
    gmask = jnp.zeros((N_GROUPS, tm), jnp.bool_)
    cur = gscore
    for _ in range(TOPK_GROUPS):
        m = jnp.max(cur, axis=0, keepdims=True)
        f = jnp.min(jnp.where(cur == m, gio, N_GROUPS), axis=0, keepdims=True)
        pick = gio == f
        gmask = jnp.logical_or(gmask, pick)
        cur = jnp.where(pick, neg, cur)
    emask = jnp.broadcast_to(gmask.reshape(N_GROUPS, 1, tm), (N_GROUPS, gs, tm))

    cur = jnp.where(emask, g3, neg)
    chosen = jnp.zeros((N_GROUPS, gs, tm), jnp.bool_)
    picks, idxs, ws = [], [], []
    for _ in range(TOP_K):
        m = jnp.max(jnp.max(cur, axis=1, keepdims=True), axis=0, keepdims=True)
        f = jnp.min(jnp.min(jnp.where(cur == m, eio, ne), axis=1, keepdims=True), axis=0, keepdims=True)
        pick = eio == f
        wk = jnp.sum(jnp.sum(jnp.where(pick, s3, 0.0), axis=1, keepdims=True), axis=0, keepdims=True)
        cur = jnp.where(pick, neg, cur)
        chosen = jnp.logical_or(chosen, pick)
        picks.append(pick)
        idxs.append(f.reshape(1, tm))
        ws.append(wk.reshape(1, tm))
    eidx_ref[...] = jnp.concatenate(idxs, axis=0)
    w = jnp.concatenate(ws, axis=0)
    ew_ref[...] = w / jnp.sum(w, axis=0, keepdims=True) * ROUTED_SCALE

    sel01 = jnp.where(chosen, 1.0, 0.0).reshape(ne, tm)
    r0 = lax.broadcasted_iota(jnp.int32, (tm, tm), 0)
    c0 = lax.broadcasted_iota(jnp.int32, (tm, tm), 1)
    before = jnp.where(r0 < c0, 1.0, 0.0).astype(BF16)
    prefix = _dot(sel01.astype(BF16), before) + carry_ref[...][:, 0:1]
    p3 = prefix.reshape(N_GROUPS, gs, tm)
    ranks = []
    for pick in picks:
        rk = jnp.sum(jnp.sum(jnp.where(pick, p3, 0.0), axis=1, keepdims=True), axis=0, keepdims=True)
        ranks.append(rk.reshape(1, tm))
    rank_ref[...] = jnp.concatenate(ranks, axis=0).astype(jnp.int32)
    total = carry_ref[...] + jnp.sum(sel01, axis=1, keepdims=True)
    carry_ref[...] = total
    cnt_ref[...] = total.astype(jnp.int32)


def _router_call(x, w_hi, w_lo, bias):
    T, D = x.shape
    ne = w_hi.shape[0]
    tm = _tile(T, 512)
    return pl.pallas_call(
        _router_kernel,
        grid=(T // tm,),
        in_specs=[pl.BlockSpec((tm, D), lambda i: (i, 0)),
                  pl.BlockSpec((ne, D), lambda i: (0, 0)),
                  pl.BlockSpec((ne, D), lambda i: (0, 0)),
                  pl.BlockSpec((ne, 128), lambda i: (0, 0))],
        out_specs=[pl.BlockSpec((TOP_K, tm), lambda i: (0, i)),
                   pl.BlockSpec((TOP_K, tm), lambda i: (0, i)),
                   pl.BlockSpec((TOP_K, tm), lambda i: (0, i)),
                   pl.BlockSpec((ne, 128), lambda i: (0, 0))],
        out_shape=[jax.ShapeDtypeStruct((TOP_K, T), jnp.int32),
                   jax.ShapeDtypeStruct((TOP_K, T), F32),
                   jax.ShapeDtypeStruct((TOP_K, T), jnp.int32),
                   jax.ShapeDtypeStruct((ne, 128), jnp.int32)],
        scratch_shapes=[pltpu.VMEM((ne, 128), F32)],
        compiler_params=_params(("arbitrary",)),
        name="router",
    )(x, w_hi, w_lo, bias)


def _row_copy(x_hbm, xbuf, sem, tok, slot, r):
    return pltpu.make_async_copy(x_hbm.at[pl.ds(tok, 1), :], xbuf.at[slot, pl.ds(r, 1), :], sem.at[slot])


def _experts_kernel(be_ref, nu_ref, cur_ref, nxt_ref, x_hbm, w1_ref, w3_ref, w2_ref, y_ref,
                    xbuf, w1b, w3b, w2b, sem):
    b = pl.program_id(0)
    rows = xbuf.shape[1]
    n_used = nu_ref[0]
    slot = b % 2

    def start(idx_ref, sl):
        def body(r, carry):
            _row_copy(x_hbm, xbuf, sem, idx_ref[0, r], sl, r).start()
            return carry
        lax.fori_loop(0, rows, body, 0, unroll=8)

    @pl.when(b == 0)
    def _():
        start(cur_ref, 0)

    @pl.when(b + 1 < n_used)
    def _():
        start(nxt_ref, 1 - slot)

    @pl.when((b == 0) | (be_ref[b] != be_ref[jnp.maximum(b - 1, 0)]))
    def _():
        w1b[...] = w1_ref[...].astype(BF16)
        w3b[...] = w3_ref[...].astype(BF16)
        w2b[...] = w2_ref[...].astype(BF16)

    @pl.when(b < n_used)
    def _():
        def body(r, carry):
            _row_copy(x_hbm, xbuf, sem, 0, slot, r).wait()
            return carry
        lax.fori_loop(0, rows, body, 0, unroll=8)
        xb = xbuf[slot].astype(BF16)
        act = _silu(_dot(xb, w1b[...])) * _dot(xb, w3b[...])
        y_ref[...] = _dot(act.astype(BF16), w2b[...])

    @pl.when(b >= n_used)
    def _():
        y_ref[...] = jnp.zeros_like(y_ref)


def _experts_call(layer, block_e, n_used, slot_tok, x, w_e1, w_e3, w_e2):
    T, D = x.shape
    n_blocks = block_e.shape[0]
    de = w_e1.shape[-1]
    rows = EXPERT_BLOCK
    idx3 = slot_tok.reshape(n_blocks, 1, rows)
    grid_spec = pltpu.PrefetchScalarGridSpec(
        num_scalar_prefetch=2,
        grid=(n_blocks,),
        in_specs=[pl.BlockSpec((None, 1, rows), lambda b, be, nu: (b, 0, 0), memory_space=pltpu.SMEM),
                  pl.BlockSpec((None, 1, rows), lambda b, be, nu: (jnp.minimum(b + 1, n_blocks - 1), 0, 0),
                               memory_space=pltpu.SMEM),
                  pl.BlockSpec(memory_space=pl.ANY),
                  pl.BlockSpec((None, None, D, de), lambda b, be, nu: (layer, be[b], 0, 0)),
                  pl.BlockSpec((None, None, D, de), lambda b, be, nu: (layer, be[b], 0, 0)),
                  pl.BlockSpec((None, None, de, D), lambda b, be, nu: (layer, be[b], 0, 0))],
        out_specs=pl.BlockSpec((rows, D), lambda b, be, nu: (b, 0)),
        scratch_shapes=[pltpu.VMEM((2, rows, D), F32),
                        pltpu.VMEM((D, de), BF16), pltpu.VMEM((D, de), BF16), pltpu.VMEM((de, D), BF16),
                        pltpu.SemaphoreType.DMA((2,))],
    )
    return pl.pallas_call(
        _experts_kernel,
        grid_spec=grid_spec,
        out_shape=jax.ShapeDtypeStruct((n_blocks * rows, D), F32),
        compiler_params=_params(("arbitrary",)),
        name="routed_experts",
    )(block_e, n_used, idx3, idx3, x, w_e1, w_e3, w_e2)


def _out_copy(y_hbm, gbuf, sem, row, slot, k, t):
    return pltpu.make_async_copy(y_hbm.at[pl.ds(row, 1), :], gbuf.at[slot, k, pl.ds(t, 1), :], sem.at[slot])


def _combine_kernel(cur_ref, nxt_ref, y_hbm, ew_ref, x_ref, ws1_ref, ws3_ref, ws2_ref, g_ref, b_ref, o_ref,
                    gbuf, sem, *, alpha):
    i = pl.program_id(0)
    n = pl.num_programs(0)
    tm = x_ref.shape[0]
    topk = gbuf.shape[1]
    slot = i % 2

    def start(idx_ref, sl):
        for k in range(topk):
            def body(t, carry):
                _out_copy(y_hbm, gbuf, sem, idx_ref[k, t], sl, k, t).start()
                return carry
            lax.fori_loop(0, tm, body, 0, unroll=8)

    @pl.when(i == 0)
    def _():
        start(cur_ref, 0)

    @pl.when(i + 1 < n)
    def _():
        start(nxt_ref, 1 - slot)

    x = x_ref[...]
    xb = x.astype(BF16)
    act = _silu(_dot(xb, ws1_ref[...])) * _dot(xb, ws3_ref[...])
    acc = _dot(act.astype(BF16), ws2_ref[...])

    def wbody(t, carry):
        for k in range(topk):
            _out_copy(y_hbm, gbuf, sem, 0, slot, k, t).wait()
        return carry
    lax.fori_loop(0, tm, wbody, 0, unroll=2)

    ew = ew_ref[...]
    for k in range(topk):
        acc = acc + gbuf[slot, k] * ew[:, k:k + 1]
    o_ref[...] = _layer_norm(alpha * x + acc, g_ref[...], b_ref[...])


def _combine_call(slot_kt, yb, ew_t, x, ws1, ws3, ws2, g, b, alpha):
    T, D = x.shape
    de = ws1.shape[1]
    tm = _tile(T, 128)
    nt = T // tm
    return pl.pallas_call(
        functools.partial(_combine_kernel, alpha=alpha),
        grid=(nt,),
        in_specs=[pl.BlockSpec((TOP_K, tm), lambda i: (0, i), memory_space=pltpu.SMEM),
                  pl.BlockSpec((TOP_K, tm), lambda i: (0, jnp.minimum(i + 1, nt - 1)), memory_space=pltpu.SMEM),
                  pl.BlockSpec(memory_space=pl.ANY),
                  pl.BlockSpec((tm, TOP_K), lambda i: (i, 0)),
                  pl.BlockSpec((tm, D), lambda i: (i, 0)),
                  pl.BlockSpec((D, de), lambda i: (0, 0)),
                  pl.BlockSpec((D, de), lambda i: (0, 0)),
                  pl.BlockSpec((de, D), lambda i: (0, 0)),
                  pl.BlockSpec((1, D), lambda i: (0, 0)),
                  pl.BlockSpec((1, D), lambda i: (0, 0))],
        out_specs=pl.BlockSpec((tm, D), lambda i: (i, 0)),
        out_shape=jax.ShapeDtypeStruct((T, D), F32),
        scratch_shapes=[pltpu.VMEM((2, TOP_K, tm, D), F32), pltpu.SemaphoreType.DMA((2,))],
        compiler_params=_params(("arbitrary",)),
        name="combine_shared_ln2",
    )(slot_kt, slot_kt, yb, ew_t, x, ws1, ws3, ws2, g.reshape(1, D), b.reshape(1, D))


def _chunk_meta(n_prompt, prompt_len, n_sample, sample_len, past_len):
    seq, pos0, first, last = [], [], [], []
    for s in range(n_prompt):
        nc = prompt_len // CHUNK
        for j in range(nc):
            seq.append(s), pos0.append(j * CHUNK), first.append(int(j == 0)), last.append(int(j == nc - 1))
    for s in range(n_sample):
        nc = sample_len // CHUNK
        for j in range(nc):
            seq.append(n_prompt + s), pos0.append(past_len + j * CHUNK)
            first.append(int(j == 0)), last.append(int(j == nc - 1))
    return tuple(jnp.asarray(np.asarray(v, np.int32)) for v in (seq, pos0, first, last))


def kernel(x_prompt, x_sample, mem_prompt, state_gla, state_pool, cache_mem_k, cache_mem_v, ln_in_g, ln_in_b, w_in, w_a2, b_a2, gla_norm_g, w_pool, pool_scale, w_mk, w_mv, w_branch, w_o, ln1_g, ln1_b, w_router, router_bias, w_e1, w_e3, w_e2, w_s1, w_s3, w_s2, ln2_g, ln2_b):
    bp, lp, D = x_prompt.shape
    bs, ls, _ = x_sample.shape
    depth, _, heads, dk, dv = state_gla.shape
    bw = heads * dv
    kw = heads * dk
    rank = w_a2.shape[1]
    hist = state_pool.shape[2]
    mem_len = mem_prompt.shape[1]
    mem_heads, mem_hd = cache_mem_k.shape[3], cache_mem_k.shape[4]
    ne = w_router.shape[2]
    past_len = 4096
    alpha = (2 * depth) ** 0.25
    tp, ts = bp * lp, bs * ls
    T = tp + ts
    assert lp % CHUNK == 0 and ls % CHUNK == 0 and hist == HIST_ROWS - 1 and rank <= A_PAD
    assert (T * TOP_K) % EXPERT_BLOCK == 0 and ne % N_GROUPS == 0

    meta = _chunk_meta(bp, lp, bs, ls, past_len)
    x = _ln_call(jnp.concatenate([x_prompt.reshape(tp, D), x_sample.reshape(ts, D)], axis=0), ln_in_g, ln_in_b)

    w_mem = jnp.concatenate([w_mk, w_mv], axis=0).astype(BF16)
    w_mem = jnp.transpose(w_mem, (1, 0, 2)).reshape(D, 2 * depth * bw)
    mem_kv = _mm_call(mem_prompt.reshape(bp * mem_len, D), w_mem, "mem_kv").reshape(bp, mem_len, 2 * depth, bw)

    offs = np.cumsum([0, kw, kw, bw, bw, rank, bw, bw, 3 * D])
    seg = lambda n: slice(int(offs[n]), int(offs[n + 1]))
    order = [7, 0, 1, 2, 3, 5, 6]
    col_blocks = (3 * D // kw, 3 * D // kw + 1, (3 * D + 2 * kw) // bw, (3 * D + 2 * kw) // bw + 1,
                  (3 * D + 2 * kw) // bw + 2, (3 * D + 2 * kw) // bw + 3)
    assert (3 * D) % kw == 0 and (3 * D + 2 * kw) % bw == 0

    n_assign = T * TOP_K
    n_blocks = n_assign // EXPERT_BLOCK + ne
    n_slots = n_blocks * EXPERT_BLOCK
    tok_of_assign = jnp.broadcast_to(jnp.arange(T, dtype=jnp.int32)[None, :], (TOP_K, T)).reshape(-1)

    gla_out, pool_out, mk_out, mv_out = [], [], [], []
    for l in range(depth):
        w_main = jnp.concatenate([w_in[l][:, seg(n)] for n in order], axis=1).astype(BF16)
        w_a = jnp.pad(w_in[l][:, seg(4)], ((0, 0), (0, A_PAD - rank))).astype(BF16)
        proj, a = _proj_call(x, w_main, w_a)

        mk_l = mem_kv[:, :, l, :]
        mv_l = mem_kv[:, :, depth + l, :]
        mk_out.append(mk_l.reshape(bp, mem_len, mem_heads, mem_hd))
        mv_out.append(mv_l.reshape(bp, mem_len, mem_heads, mem_hd))
        mk_all = jnp.concatenate([mk_l, cache_mem_k[l].reshape(bs, mem_len, bw)], axis=0).astype(BF16)
        mv_all = jnp.concatenate([mv_l, cache_mem_v[l].reshape(bs, mem_len, bw)], axis=0).astype(BF16)
        s0 = jnp.concatenate([jnp.zeros((bp, heads, dk, dv), F32), state_gla[l]], axis=0)
        h0 = jnp.concatenate([jnp.zeros((bp, hist, bw), F32), state_pool[l]], axis=0)
        h0 = jnp.pad(h0, ((0, 0), (HIST_ROWS - hist, 0), (0, 0)))
        wa2 = jnp.pad(w_a2[l], ((0, A_PAD - rank), (0, 0))).astype(BF16)
        br, s_new, h_new = _mixer_call(
            meta, proj, a, s0, h0, mk_all, mv_all, wa2, b_a2[l].reshape(1, kw), gla_norm_g[l].reshape(1, bw),
            w_pool[l].astype(BF16), pool_scale[l].reshape(1, bw), col_blocks, mem_heads)
        gla_out.append(s_new)
        pool_out.append(h_new[:, HIST_ROWS - hist:, :])

        x = _merge_call(br, proj, x, w_branch[l].astype(BF16), w_o[l].astype(BF16), ln1_g[l], ln1_b[l], alpha)

        wr_hi, wr_lo = _split(w_router[l].T)
        bias = jnp.broadcast_to(router_bias[l].reshape(ne, 1), (ne, 128))
        eidx, ew, rk, cnt = _router_call(x, wr_hi, wr_lo, bias)

        counts = cnt[:, 0]
        padded = (counts + EXPERT_BLOCK - 1) // EXPERT_BLOCK * EXPERT_BLOCK
        pend = jnp.cumsum(padded)
        pstart = pend - padded
        slot_kt = pstart[eidx] + rk
        slot_tok = jnp.zeros((n_slots,), jnp.int32).at[slot_kt.reshape(-1)].set(tok_of_assign)
        blk0 = jnp.arange(n_blocks, dtype=jnp.int32) * EXPERT_BLOCK
        block_e = jnp.minimum(jnp.searchsorted(pend, blk0, side='right'), ne - 1).astype(jnp.int32)
        n_used = (pend[-1] // EXPERT_BLOCK).astype(jnp.int32).reshape(1)
        block_e = jnp.where(blk0 < pend[-1], block_e, block_e[jnp.maximum(n_used[0] - 1, 0)])

        yb = _experts_call(l, block_e, n_used, slot_tok, x, w_e1, w_e3, w_e2)
        x = _combine_call(slot_kt, yb, ew.T, x, w_s1[l].astype(BF16), w_s3[l].astype(BF16), w_s2[l].astype(BF16),
                          ln2_g[l], ln2_b[l], alpha)

    gla_all = jnp.stack(gla_out)
    pool_all = jnp.stack(pool_out)
    return (x[:tp].reshape(bp, lp, D), x[tp:].reshape(bs, ls, D),
            gla_all[:, :bp], pool_all[:, :bp], jnp.stack(mk_out), jnp.stack(mv_out),
            gla_all[:, bp:], pool_all[:, bp:])
```
